```python
import math
import jax, jax.numpy as jnp
from jax import lax
import numpy as np

D_MODEL = 1024
BATCH = 16
SEQ = 2048
DEPTH = 1
DEC_BATCH = 32
DEC_SEQ = 4
PAST_LEN = 16384
PAGE_SIZE = 128

N_HEADS = 8
HEAD_DIM = 64
V_DIM = 2 * HEAD_DIM
QK_W = N_HEADS * 2 * HEAD_DIM
V_W = N_HEADS * V_DIM
CONV_CH = D_MODEL
CONV_WIDTH = 31
IN_W = 2 * QK_W + V_W + 2 * CONV_CH + 2 * D_MODEL
N_GROUPS = 4
EXPERTS_PER_GROUP = 8
N_EXPERTS = N_GROUPS * EXPERTS_PER_GROUP
TOP_K = 2
EXPERT_FF = D_MODEL // 2
ROPE_THETA = 10000.0
Q_BLOCK = 128
EPS = 1e-6
LAMBDA_STD = 0.1

kernel_name = 'hybrid_diffattn_conformer_hmoe_step'

F32 = jnp.float32


def rms_norm(x, g):
    xf = x.astype(F32)
    y = xf * lax.rsqrt(jnp.mean(xf * xf, axis=-1, keepdims=True) + EPS)
    return (y * g.astype(F32)).astype(x.dtype)


def layer_norm(x, g, b):
    xf = x.astype(F32)
    mu = jnp.mean(xf, axis=-1, keepdims=True)
    var = jnp.mean(jnp.square(xf - mu), axis=-1, keepdims=True)
    y = (xf - mu) * lax.rsqrt(var + EPS)
    return (y * g.astype(F32) + b.astype(F32)).astype(x.dtype)


def rotary(x, pos):
    half = HEAD_DIM // 2
    inv = 1.0 / (ROPE_THETA ** (jnp.arange(half, dtype=F32) * 2.0 / HEAD_DIM))
    ang = pos[:, None] * inv[None, :]
    cos = jnp.cos(ang)[None, :, None, None, :]
    sin = jnp.sin(ang)[None, :, None, None, :]
    xf = x.astype(F32)
    x1, x2 = xf[..., :half], xf[..., half:]
    return jnp.concatenate([x1 * cos - x2 * sin, x2 * cos + x1 * sin], axis=-1).astype(x.dtype)


def diff_lambda(lq1, lk1, lq2, lk2, lam_init):
    e1 = jnp.exp(jnp.sum(lq1.astype(F32) * lk1.astype(F32)))
    e2 = jnp.exp(jnp.sum(lq2.astype(F32) * lk2.astype(F32)))
    return e1 - e2 + lam_init


def prompt_diff_attn(q, k, v, lam):
    B, S = q.shape[0], q.shape[1]
    nb = S // Q_BLOCK
    scale = HEAD_DIM ** -0.5
    q_blocks = q.reshape(B, nb, Q_BLOCK, N_HEADS, 2, HEAD_DIM).swapaxes(0, 1)
    k_pos = jnp.arange(S)

    def block(args):
        qb, i = args
        s = jnp.einsum('bqhmd,bkhmd->bhmqk', qb, k, preferred_element_type=F32) * scale
        q_pos = i * Q_BLOCK + jnp.arange(Q_BLOCK)
        s = jnp.where(k_pos[None, :] <= q_pos[:, None], s, -jnp.inf)
        p = jax.nn.softmax(s, axis=-1)
        o = jnp.einsum('bhmqk,bkhe->bhmqe', p.astype(v.dtype), v, preferred_element_type=F32)
        return (o[:, :, 0] - lam * o[:, :, 1]).transpose(0, 2, 1, 3)

    d = lax.map(block, (q_blocks, jnp.arange(nb)))
    return d.swapaxes(0, 1).reshape(B, S, N_HEADS, V_DIM)


def online_update(carry, s, v):
    m, l, acc = carry
    m_new = jnp.maximum(m, jnp.max(s, axis=-1))
    p = jnp.exp(s - m_new[..., None])
    corr = jnp.exp(m - m_new)
    pv = jnp.einsum('bhmqk,bkhe->bhmqe', p.astype(v.dtype), v, preferred_element_type=F32)
    return (m_new, l * corr + jnp.sum(p, axis=-1), acc * corr[..., None] + pv)


def sample_diff_attn(q, k_new, v_new, cache_k, cache_v, layer, page_table, lam):
    Bd, Q = q.shape[0], q.shape[1]
    scale = HEAD_DIM ** -0.5
    init = (jnp.full((Bd, N_HEADS, 2, Q), -jnp.inf, F32),
            jnp.zeros((Bd, N_HEADS, 2, Q), F32),
            jnp.zeros((Bd, N_HEADS, 2, Q, V_DIM), F32))

    def page_step(carry, pages):
        kp = cache_k[layer, pages].reshape(Bd, PAGE_SIZE, N_HEADS, 2, HEAD_DIM)
        vp = cache_v[layer, pages]
        s = jnp.einsum('bqhmd,bkhmd->bhmqk', q, kp, preferred_element_type=F32) * scale
        return online_update(carry, s, vp), None

    carry, _ = lax.scan(page_step, init, page_table.T)
    s = jnp.einsum('bqhmd,bkhmd->bhmqk', q, k_new, preferred_element_type=F32) * scale
    causal = jnp.tril(jnp.ones((Q, Q), dtype=bool))
    s = jnp.where(causal, s, -jnp.inf)
    m, l, acc = online_update(carry, s, v_new)
    o = acc / l[..., None]
    return (o[:, :, 0] - lam * o[:, :, 1]).transpose(0, 2, 1, 3)


def conv_module(u, buf, conv_w, conv_b, g, b, w_pb, b_pb):
    xpad = jnp.concatenate([buf.astype(u.dtype), u], axis=1)
    y = lax.conv_general_dilated(xpad, conv_w.astype(u.dtype)[:, None, :], window_strides=(1,),
                                 padding='VALID', dimension_numbers=('NWC', 'WIO', 'NWC'),
                                 feature_group_count=CONV_CH)
    y = jax.nn.silu(layer_norm(y + conv_b, g, b))
    return y @ w_pb + b_pb, xpad[:, -(CONV_WIDTH - 1):]


def mixer_sublayer(x, pos, attend, conv_buf, lam_init, norm1_g, w_in, b_gate, attn_norm_g, w_pa,
                   conv_w, conv_b, conv_norm_g, conv_norm_b, w_pb, b_pb, w_o):
    B, T, _ = x.shape
    h = rms_norm(x, norm1_g)
    proj = h @ w_in
    o_k, o_v, o_c = QK_W, 2 * QK_W, 2 * QK_W + V_W
    o_g = o_c + 2 * CONV_CH
    q = proj[..., :o_k].reshape(B, T, N_HEADS, 2, HEAD_DIM)
    k = proj[..., o_k:o_v].reshape(B, T, N_HEADS, 2, HEAD_DIM)
    v = proj[..., o_v:o_c].reshape(B, T, N_HEADS, V_DIM)
    u = proj[..., o_c:o_c + CONV_CH] * jax.nn.sigmoid(proj[..., o_c + CONV_CH:o_g])
    gates = jax.nn.sigmoid(proj[..., o_g:] + b_gate)
    q, k = rotary(q, pos), rotary(k, pos)
    d = attend(q, k, v)
    d = (rms_norm(d, attn_norm_g) * (1.0 - lam_init)).astype(x.dtype)
    y_a = d.reshape(B, T, V_W) @ w_pa
    y_b, new_buf = conv_module(u, conv_buf, conv_w, conv_b, conv_norm_g, conv_norm_b, w_pb, b_pb)
    merged = gates[..., :D_MODEL] * y_a + gates[..., D_MODEL:] * y_b
    out = x + (merged @ w_o).astype(x.dtype)
    return out, k.reshape(B, T, N_HEADS, 2 * HEAD_DIM), v, new_buf


def moe_ffn(x, w_rg, b_rg, w_re, b_re, w_gate, w_up, w_down):
    B, T, D = x.shape
    xt = x.reshape(B * T, D)
    gl = (xt @ w_rg).astype(F32) + b_rg.astype(F32)
    gp = jax.nn.softmax(gl, axis=-1)
    g_idx = jnp.argmax(gl, axis=-1)
    g_w = jnp.take_along_axis(gp, g_idx[:, None], axis=1)[:, 0]
    el = ((xt @ w_re).astype(F32) + b_re.astype(F32)).reshape(B * T, N_GROUPS, EXPERTS_PER_GROUP)
    el_sel = jnp.take_along_axis(el, g_idx[:, None, None], axis=1)[:, 0]
    top_v, top_i = lax.top_k(el_sel, TOP_K)
    w = g_w[:, None] * jax.nn.softmax(top_v, axis=-1)
    eid = g_idx[:, None] * EXPERTS_PER_GROUP + top_i
    gates = jnp.sum(jax.nn.one_hot(eid, N_EXPERTS, dtype=F32) * w[..., None], axis=1)
    out = jnp.zeros((B * T, D), F32)
    for e in range(N_EXPERTS):
        hh = jax.nn.silu(xt @ w_gate[e]) * (xt @ w_up[e])
        out = out + gates[:, e:e + 1] * (hh @ w_down[e]).astype(F32)
    return out.astype(x.dtype).reshape(B, T, D)


def setup_inputs(seed: int = 0) -> dict:
    key = jax.random.key(seed)
    ks = jax.random.split(key, 32)
    n_pages = PAST_LEN // PAGE_SIZE
    n_pool = (DEC_BATCH * n_pages * 5) // 4

    def nrm(k, shape, scale):
        return jax.random.normal(k, shape, F32) * scale

    def gain(k, shape):
        return 1.0 + nrm(k, shape, 0.02)

    perm = jax.random.permutation(ks[5], n_pool)
    page_table = perm[:DEC_BATCH * n_pages].reshape(DEC_BATCH, n_pages).astype(jnp.int32)
    return {
        'x_prompt': nrm(ks[0], (BATCH, SEQ, D_MODEL), 1.0),
        'x_sample': nrm(ks[1], (DEC_BATCH, DEC_SEQ, D_MODEL), 1.0),
        'cache_k': nrm(ks[2], (DEPTH, n_pool, PAGE_SIZE, N_HEADS, 2 * HEAD_DIM), 1.0),
        'cache_v': nrm(ks[3], (DEPTH, n_pool, PAGE_SIZE, N_HEADS, V_DIM), 1.0),
        'state_conv': nrm(ks[4], (DEPTH, DEC_BATCH, CONV_WIDTH - 1, CONV_CH), 0.5),
        'page_table': page_table,
        'norm1_g': gain(ks[6], (DEPTH, D_MODEL)),
        'w_in': nrm(ks[7], (DEPTH, D_MODEL, IN_W), D_MODEL ** -0.5),
        'b_gate': nrm(ks[8], (DEPTH, 2 * D_MODEL), 0.1),
        'lambda_q1': nrm(ks[9], (DEPTH, HEAD_DIM), LAMBDA_STD),
        'lambda_k1': nrm(ks[10], (DEPTH, HEAD_DIM), LAMBDA_STD),
        'lambda_q2': nrm(ks[11], (DEPTH, HEAD_DIM), LAMBDA_STD),
        'lambda_k2': nrm(ks[12], (DEPTH, HEAD_DIM), LAMBDA_STD),
        'attn_norm_g': gain(ks[13], (DEPTH, V_DIM)),
        'w_pa': nrm(ks[14], (DEPTH, V_W, D_MODEL), V_W ** -0.5),
        'conv_w': nrm(ks[15], (DEPTH, CONV_WIDTH, CONV_CH), CONV_WIDTH ** -0.5),
        'conv_b': nrm(ks[16], (DEPTH, CONV_CH), 0.02),
        'conv_norm_g': gain(ks[17], (DEPTH, CONV_CH)),
        'conv_norm_b': nrm(ks[18], (DEPTH, CONV_CH), 0.02),
        'w_pb': nrm(ks[19], (DEPTH, CONV_CH, D_MODEL), CONV_CH ** -0.5),
        'b_pb': nrm(ks[20], (DEPTH, D_MODEL), 0.02),
        'w_o': nrm(ks[21], (DEPTH, D_MODEL, D_MODEL), D_MODEL ** -0.5),
        'norm2_g': gain(ks[22], (DEPTH, D_MODEL)),
        'w_router_group': nrm(ks[23], (DEPTH, D_MODEL, N_GROUPS), D_MODEL ** -0.5),
        'b_router_group': nrm(ks[24], (DEPTH, N_GROUPS), 0.01),
        'w_router_expert': nrm(ks[25], (DEPTH, D_MODEL, N_EXPERTS), D_MODEL ** -0.5),
        'b_router_expert': nrm(ks[26], (DEPTH, N_EXPERTS), 0.01),
        'w_exp_gate': nrm(ks[27], (DEPTH, N_EXPERTS, D_MODEL, EXPERT_FF), D_MODEL ** -0.5),
        'w_exp_up': nrm(ks[28], (DEPTH, N_EXPERTS, D_MODEL, EXPERT_FF), D_MODEL ** -0.5),
        'w_exp_down': nrm(ks[29], (DEPTH, N_EXPERTS, EXPERT_FF, D_MODEL), EXPERT_FF ** -0.5),
        'final_norm_g': gain(ks[30], (D_MODEL,)),
    }


def reference(x_prompt, x_sample, cache_k, cache_v, state_conv, page_table, norm1_g, w_in, b_gate,
              lambda_q1, lambda_k1, lambda_q2, lambda_k2, attn_norm_g, w_pa, conv_w, conv_b,
              conv_norm_g, conv_norm_b, w_pb, b_pb, w_o, norm2_g, w_router_group, b_router_group,
              w_router_expert, b_router_expert, w_exp_gate, w_exp_up, w_exp_down, final_norm_g):
    past_len = page_table.shape[1] * PAGE_SIZE
    pos_p = jnp.arange(x_prompt.shape[1], dtype=F32)
    pos_s = past_len + jnp.arange(x_sample.shape[1], dtype=F32)
    h_p, h_s = x_prompt, x_sample
    kps, vps, cps, kss, vss, css = [], [], [], [], [], []
    for l in range(DEPTH):
        lam_init = 0.8 - 0.6 * math.exp(-0.3 * l)
        lam = diff_lambda(lambda_q1[l], lambda_k1[l], lambda_q2[l], lambda_k2[l], lam_init)
        zero_buf = jnp.zeros((h_p.shape[0], CONV_WIDTH - 1, CONV_CH), h_p.dtype)
        attend_p = lambda q, k, v, lam=lam: prompt_diff_attn(q, k, v, lam)
        attend_s = lambda q, k, v, lam=lam, l=l: sample_diff_attn(q, k, v, cache_k, cache_v, l, page_table, lam)
        h_p, kp, vp, cp = mixer_sublayer(h_p, pos_p, attend_p, zero_buf, lam_init, norm1_g[l], w_in[l], b_gate[l],
                                         attn_norm_g[l], w_pa[l], conv_w[l], conv_b[l], conv_norm_g[l],
                                         conv_norm_b[l], w_pb[l], b_pb[l], w_o[l])
        h_s, ks_, vs_, cs_ = mixer_sublayer(h_s, pos_s, attend_s, state_conv[l], lam_init, norm1_g[l], w_in[l],
                                            b_gate[l], attn_norm_g[l], w_pa[l], conv_w[l], conv_b[l],
                                            conv_norm_g[l], conv_norm_b[l], w_pb[l], b_pb[l], w_o[l])
        h_p = h_p + moe_ffn(rms_norm(h_p, norm2_g[l]), w_router_group[l], b_router_group[l], w_router_expert[l],
                            b_router_expert[l], w_exp_gate[l], w_exp_up[l], w_exp_down[l])
        h_s = h_s + moe_ffn(rms_norm(h_s, norm2_g[l]), w_router_group[l], b_router_group[l], w_router_expert[l],
                            b_router_expert[l], w_exp_gate[l], w_exp_up[l], w_exp_down[l])
        kps.append(kp); vps.append(vp); cps.append(cp)
        kss.append(ks_); vss.append(vs_); css.append(cs_)
    y_prompt = rms_norm(h_p, final_norm_g)
    y_sample = rms_norm(h_s, final_norm_g)
    k_prompt, v_prompt, conv_prompt = jnp.stack(kps), jnp.stack(vps), jnp.stack(cps)
    k_sample, v_sample, conv_sample = jnp.stack(kss), jnp.stack(vss), jnp.stack(css)
    return (y_prompt, y_sample, k_prompt, v_prompt, conv_prompt, k_sample, v_sample, conv_sample)
```

```python
import functools
import math

import jax
import jax.numpy as jnp
from jax import lax
from jax.experimental import pallas as pl
from jax.experimental.pallas import tpu as pltpu

F32 = jnp.float32
BF16 = jnp.bfloat16
I32 = jnp.int32

D_MODEL = 1024
N_HEADS = 8
HEAD_DIM = 64
V_DIM = 2 * HEAD_DIM
QK_W = N_HEADS * 2 * HEAD_DIM
V_W = N_HEADS * V_DIM
CONV_CH = D_MODEL
CONV_WIDTH = 31
N_GROUPS = 4
EXPERTS_PER_GROUP = 8
N_EXPERTS = N_GROUPS * EXPERTS_PER_GROUP
EXPERT_FF = D_MODEL // 2
ROPE_THETA = 10000.0
PAGE_SIZE = 128
EPS = 1e-6
QK_SCALE = HEAD_DIM ** -0.5

LANES = 128
SUBLANES = 8
VMEM_LIMIT = 56 * 1024 * 1024

TM_IN = 512
TQ = 256
TC = 256
HALO = 32
RC = 32
TME = 256
TME_LOG = 8
TP = 256
TS = 512
TF = 256
PG = 8


def _cparams(n_axes):
    return pltpu.CompilerParams(
        dimension_semantics=("arbitrary",) * n_axes, vmem_limit_bytes=VMEM_LIMIT)


def _sigmoid(x):
    return 1.0 / (1.0 + jnp.exp(-x))


def _dot(a, b):
    return jnp.dot(a, b, preferred_element_type=F32)


def _dot_nt(a, b):
    return lax.dot_general(a, b, (((1,), (1,)), ((), ())), preferred_element_type=F32)


def _inproj_kernel(x_ref, g1_ref, w_ref, bg_ref, cos_ref, sa_ref, sb_ref,
                   q_ref, k_ref, v_ref, u_ref, gate_ref):
    x = x_ref[...]
    ms = jnp.mean(x * x, axis=-1, keepdims=True)
    h = (x * lax.rsqrt(ms + EPS) * g1_ref[...]).astype(BF16)
    cos = cos_ref[...]
    sa = sa_ref[...]
    sb = sb_ref[...]

    def rot(xs):
        return xs * cos + pltpu.roll(xs, 96, 1) * sa + pltpu.roll(xs, 32, 1) * sb

    pq = _dot(h, w_ref[:, 0:QK_W])
    for hh in range(N_HEADS):
        sl = slice(hh * LANES, (hh + 1) * LANES)
        q_ref[:, sl] = (rot(pq[:, sl]) * QK_SCALE).astype(BF16)
    pk = _dot(h, w_ref[:, QK_W:2 * QK_W])
    for hh in range(N_HEADS):
        sl = slice(hh * LANES, (hh + 1) * LANES)
        k_ref[:, sl] = rot(pk[:, sl])
    o_v, o_c = 2 * QK_W, 2 * QK_W + V_W
    v_ref[...] = _dot(h, w_ref[:, o_v:o_c])
    pa = _dot(h, w_ref[:, o_c:o_c + CONV_CH])
    pb = _dot(h, w_ref[:, o_c + CONV_CH:o_c + 2 * CONV_CH])
    u_ref[...] = pa * _sigmoid(pb)
    o_g = o_c + 2 * CONV_CH
    pg = _dot(h, w_ref[:, o_g:o_g + 2 * D_MODEL]) + bg_ref[...]
    gate_ref[...] = _sigmoid(pg).astype(BF16)


def _inproj(x2d, g1, w_in_bf, b_gate, cos_t, sa_t, sb_t, tm, table_blocks):
    n = x2d.shape[0]
    in_w = w_in_bf.shape[1]
    row = lambda i: (i, 0)
    const = lambda i: (0, 0)
    tab = lambda i: (i % table_blocks, 0)
    return pl.pallas_call(
        _inproj_kernel,
        grid=(n // tm,),
        in_specs=[
            pl.BlockSpec((tm, D_MODEL), row),
            pl.BlockSpec((1, D_MODEL), const),
            pl.BlockSpec((D_MODEL, in_w), const, pipeline_mode=pl.Buffered(1)),
            pl.BlockSpec((1, 2 * D_MODEL), const),
            pl.BlockSpec((tm, LANES), tab),
            pl.BlockSpec((tm, LANES), tab),
            pl.BlockSpec((tm, LANES), tab),
        ],
        out_specs=[
            pl.BlockSpec((tm, QK_W), row),
            pl.BlockSpec((tm, QK_W), row),
            pl.BlockSpec((tm, V_W), row),
            pl.BlockSpec((tm, CONV_CH), row),
            pl.BlockSpec((tm, 2 * D_MODEL), row),
        ],
        out_shape=[
            jax.ShapeDtypeStruct((n, QK_W), BF16),
            jax.ShapeDtypeStruct((n, QK_W), F32),
            jax.ShapeDtypeStruct((n, V_W), F32),
            jax.ShapeDtypeStruct((n, CONV_CH), F32),
            jax.ShapeDtypeStruct((n, 2 * D_MODEL), BF16),
        ],
        compiler_params=_cparams(1),
        name="inproj",
    )(x2d, g1, w_in_bf, b_gate, cos_t, sa_t, sb_t)


def _rope_tables(pos):
    half = HEAD_DIM // 2
    inv = 1.0 / (ROPE_THETA ** (jnp.arange(half, dtype=F32) * 2.0 / HEAD_DIM))
    ang = pos[:, None] * inv[None, :]
    cos = jnp.tile(jnp.cos(ang), (1, LANES // half))
    sin = jnp.tile(jnp.sin(ang), (1, LANES // half))
    second_half = ((jnp.arange(LANES) % HEAD_DIM) // half)[None, :] == 1
    sa = jnp.where(second_half, 0.0, -sin)
    sb = jnp.where(second_half, sin, 0.0)
    return cos, sa, sb


def _diff_lambda_vec(lq1_ref, lk1_ref, lq2_ref, lk2_ref, lam_init):
    e1 = jnp.exp(jnp.sum(lq1_ref[...] * lk1_ref[...], axis=-1, keepdims=True))
    e2 = jnp.exp(jnp.sum(lq2_ref[...] * lk2_ref[...], axis=-1, keepdims=True))
    return e1 - e2 + lam_init


def _head_norm(d, g, lam_init):
    ms = jnp.mean(d * d, axis=-1, keepdims=True)
    return d * lax.rsqrt(ms + EPS) * g * (1.0 - lam_init)


def _prompt_attn_kernel(lq1_ref, lk1_ref, lq2_ref, lk2_ref, g_ref, mask_ref, q_ref, k_ref, v_ref,
                        o_ref, kb_ref, vb_ref, *, seq, lam_init):
    kb_ref[...] = k_ref[...].astype(BF16)
    vb_ref[...] = v_ref[...].astype(BF16)
    lam = _diff_lambda_vec(lq1_ref, lk1_ref, lq2_ref, lk2_ref, lam_init)
    g = g_ref[...]
    row = lax.broadcasted_iota(I32, (TQ, TQ), 0)
    col = lax.broadcasted_iota(I32, (TQ, TQ), 1)
    causal = col <= row
    for i in range(seq // TQ):
        q = q_ref[i * TQ:(i + 1) * TQ, :]
        lo = i * TQ
        k_diag = kb_ref[lo:lo + TQ, :]
        v_diag = vb_ref[lo:lo + TQ, :]
        outs = []
        for m in range(2):
            qm = q * mask_ref[m]
            s_d = jnp.where(causal, _dot_nt(qm, k_diag), -jnp.inf)
            mx = jnp.max(s_d, axis=-1, keepdims=True)
            if i > 0:
                s_o = _dot_nt(qm, kb_ref[0:lo, :])
                mx = jnp.maximum(mx, jnp.max(s_o, axis=-1, keepdims=True))
            p_d = jnp.exp(s_d - mx)
            l = jnp.sum(p_d, axis=-1, keepdims=True)
            o = _dot(p_d.astype(BF16), v_diag)
            if i > 0:
                p_o = jnp.exp(s_o - mx)
                l = l + jnp.sum(p_o, axis=-1, keepdims=True)
                o = o + _dot(p_o.astype(BF16), vb_ref[0:lo, :])
            outs.append(o / l)
        d = outs[0] - lam * outs[1]
        o_ref[lo:lo + TQ, :] = _head_norm(d, g, lam_init).astype(BF16)


def _prompt_attn(lams, g, q3, k3, v3, lam_init):
    b, s, _ = q3.shape
    blk = lambda bi, hi: (bi, 0, hi)
    const = lambda bi, hi: (0, 0)
    lam_spec = pl.BlockSpec((1, HEAD_DIM), const)
    lane_map = jnp.arange(LANES) // HEAD_DIM
    map_mask = jnp.broadcast_to((lane_map[None, None, :] == jnp.arange(2)[:, None, None]),
                                (2, TQ, LANES)).astype(BF16)
    return pl.pallas_call(
        functools.partial(_prompt_attn_kernel, seq=s, lam_init=lam_init),
        grid=(b, N_HEADS),
        in_specs=[lam_spec, lam_spec, lam_spec, lam_spec,
                  pl.BlockSpec((1, V_DIM), const),
                  pl.BlockSpec((2, TQ, LANES), lambda bi, hi: (0, 0, 0)),
                  pl.BlockSpec((None, s, LANES), blk),
                  pl.BlockSpec((None, s, LANES), blk),
                  pl.BlockSpec((None, s, LANES), blk)],
        out_specs=pl.BlockSpec((None, s, LANES), blk),
        out_shape=jax.ShapeDtypeStruct((b, s, V_W), BF16),
        scratch_shapes=[pltpu.VMEM((s, LANES), BF16), pltpu.VMEM((s, LANES), BF16)],
        compiler_params=_cparams(2),
        name="prompt_attn",
    )(*lams, g, map_mask, q3, k3, v3)


def _sample_attn_kernel(pt_ref, lq1_ref, lk1_ref, lq2_ref, lk2_ref, g_ref, qbd_ref,
                        knew_ref, vnew_ref, *rest, n_steps, dec_seq, lam_init):
    k_refs = rest[:PG]
    v_refs = rest[PG:2 * PG]
    o_ref = rest[2 * PG]
    qf_ref, m_ref, l_ref, acc_ref = rest[2 * PG + 1:]
    step = pl.program_id(1)
    n_rows = 2 * dec_seq * N_HEADS

    @pl.when(step == 0)
    def _():
        qf_ref[...] = qbd_ref[...].astype(F32)
        m_ref[...] = jnp.full(m_ref.shape, -jnp.inf, F32)
        l_ref[...] = jnp.zeros(l_ref.shape, F32)
        acc_ref[...] = jnp.zeros(acc_ref.shape, F32)

    qf = qf_ref[...]

    def update(s_parts, v_parts):
        m_old = m_ref[...]
        mx = m_old
        for s in s_parts:
            mx = jnp.maximum(mx, jnp.max(s, axis=-1, keepdims=True))
        corr = jnp.exp(m_old - mx)
        l = l_ref[...] * corr
        acc = acc_ref[...] * corr
        for s, v in zip(s_parts, v_parts):
            p = jnp.exp(s - mx)
            l = l + jnp.sum(p, axis=-1, keepdims=True)
            acc = acc + _dot(p, v)
        m_ref[...] = mx
        l_ref[...] = l
        acc_ref[...] = acc

    update([_dot_nt(qf, k_refs[i][...]) for i in range(PG)],
           [v_refs[i][...] for i in range(PG)])

    @pl.when(step == n_steps - 1)
    def _():
        k_new = knew_ref[...]
        v_new = vnew_ref[...]
        s_new = _dot_nt(qf, k_new)
        r = lax.broadcasted_iota(I32, (n_rows, SUBLANES), 0)
        j = lax.broadcasted_iota(I32, (n_rows, SUBLANES), 1)
        qi = (r % (dec_seq * N_HEADS)) // N_HEADS
        s_new = jnp.where(j <= qi, s_new, -jnp.inf)
        update([s_new], [v_new])
        o = acc_ref[...] / l_ref[...]
        lam = _diff_lambda_vec(lq1_ref, lk1_ref, lq2_ref, lk2_ref, lam_init)
        sub = lax.broadcasted_iota(I32, (N_HEADS, D_MODEL), 0)
        lane_head = lax.broadcasted_iota(I32, (N_HEADS, D_MODEL), 1) // V_DIM
        own_head = sub == lane_head
        g = g_ref[...]
        d_rows = []
        for t in range(dec_seq):
            rows = []
            for m in range(2):
                r0 = m * dec_seq * N_HEADS + t * N_HEADS
                blk = jnp.where(own_head, o[r0:r0 + N_HEADS, :], 0.0)
                rows.append(jnp.sum(blk, axis=0, keepdims=True))
            d_rows.append(rows[0] - lam * rows[1])
        d_rows += [jnp.zeros((1, V_W), F32)] * (SUBLANES - dec_seq)
        d = jnp.concatenate(d_rows, axis=0)
        o_ref[...] = jnp.concatenate(
            [_head_norm(d[:, hh * V_DIM:(hh + 1) * V_DIM], g, lam_init)
             for hh in range(N_HEADS)], axis=1)


def _sample_attn(page_table, lams, g, qbd, k_new3, v_new3, cache_k3, cache_v3, dec_seq,
                 lam_init):
    bd, n_pages = page_table.shape
    n_rows = qbd.shape[1]
    n_steps = n_pages // PG
    pt_flat = page_table.reshape(-1)
    const = lambda bi, si, pt: (0, 0)
    per_b = lambda bi, si, pt: (bi, 0, 0)
    lam_spec = pl.BlockSpec((1, HEAD_DIM), const)

    def page_spec(i):
        return pl.BlockSpec(
            (None, PAGE_SIZE, D_MODEL),
            lambda bi, si, pt, i=i: (pt[bi * n_pages + si * PG + i], 0, 0))

    grid_spec = pltpu.PrefetchScalarGridSpec(
        num_scalar_prefetch=1,
        grid=(bd, n_steps),
        in_specs=[lam_spec, lam_spec, lam_spec, lam_spec,
                  pl.BlockSpec((1, V_DIM), const),
                  pl.BlockSpec((None, n_rows, D_MODEL), per_b),
                  pl.BlockSpec((None, SUBLANES, D_MODEL), per_b),
                  pl.BlockSpec((None, SUBLANES, D_MODEL), per_b)]
                 + [page_spec(i) for i in range(PG)]
                 + [page_spec(i) for i in range(PG)],
        out_specs=pl.BlockSpec((None, SUBLANES, V_W), per_b),
        scratch_shapes=[pltpu.VMEM((n_rows, D_MODEL), F32),
                        pltpu.VMEM((n_rows, 1), F32),
                        pltpu.VMEM((n_rows, 1), F32),
                        pltpu.VMEM((n_rows, V_W), F32)],
    )
    return pl.pallas_call(
        functools.partial(_sample_attn_kernel, n_steps=n_steps, dec_seq=dec_seq,
                          lam_init=lam_init),
        grid_spec=grid_spec,
        out_shape=jax.ShapeDtypeStruct((bd, SUBLANES, V_W), F32),
        compiler_params=_cparams(2),
        name="sample_attn",
    )(pt_flat, *lams, g, qbd, k_new3, v_new3, *([cache_k3] * PG), *([cache_v3] * PG))


def _route(logits):
    shape = logits.shape
    lane = lax.broadcasted_iota(I32, shape, 1)
    lane_f = lane.astype(F32)
    none = float(LANES)
    neg = -jnp.inf
    is_group = lane < N_GROUPS
    gl = jnp.where(is_group, logits, neg)
    gmax = jnp.max(gl, axis=-1, keepdims=True)
    gidx = jnp.min(jnp.where(gl == gmax, lane_f, none), axis=-1, keepdims=True).astype(I32)
    gsum = jnp.sum(jnp.where(is_group, jnp.exp(logits - gmax), 0.0), axis=-1, keepdims=True)
    g_w = 1.0 / gsum
    in_group = ((lane >= N_GROUPS) & (lane < N_GROUPS + N_EXPERTS)
                & (((lane - N_GROUPS) >> 3) == gidx))
    el = jnp.where(in_group, logits, neg)
    v0 = jnp.max(el, axis=-1, keepdims=True)
    i0 = jnp.min(jnp.where(el == v0, lane_f, none), axis=-1, keepdims=True)
    el2 = jnp.where(lane_f == i0, neg, el)
    v1 = jnp.max(el2, axis=-1, keepdims=True)
    i1 = jnp.min(jnp.where(el2 == v1, lane_f, none), axis=-1, keepdims=True)
    e = jnp.exp(v1 - v0)
    den = 1.0 + e
    w0 = g_w / den
    w1 = g_w * e / den
    eid0 = i0 - float(N_GROUPS)
    eid1 = i1 - float(N_GROUPS)
    return jnp.where(lane == 0, eid0,
                     jnp.where(lane == 1, eid1,
                               jnp.where(lane == 2, w0,
                                         jnp.where(lane == 3, w1, 0.0))))


def _mixer_tail(conv_y, d, gates, x, wpa_ref, wpb_ref, wo_ref, cg_ref, cbn_ref, bpb_ref,
                g2_ref, wrh_ref, wrl_ref, br_ref):
    mu = jnp.mean(conv_y, axis=-1, keepdims=True)
    yc = conv_y - mu
    var = jnp.mean(yc * yc, axis=-1, keepdims=True)
    yn = yc * lax.rsqrt(var + EPS) * cg_ref[...] + cbn_ref[...]
    act = yn * _sigmoid(yn)
    y_b = _dot(act.astype(BF16), wpb_ref[...]) + bpb_ref[...]
    y_a = _dot(d.astype(BF16), wpa_ref[...])
    merged = gates[:, :D_MODEL].astype(F32) * y_a + gates[:, D_MODEL:].astype(F32) * y_b
    out = x + _dot(merged.astype(BF16), wo_ref[...])
    ms = jnp.mean(out * out, axis=-1, keepdims=True)
    xn = out * lax.rsqrt(ms + EPS) * g2_ref[...]
    xh = xn.astype(BF16)
    xl = (xn - xh.astype(F32)).astype(BF16)
    logits = (_dot(xh, wrh_ref[...]) + _dot(xl, wrh_ref[...]) + _dot(xh, wrl_ref[...])
              + br_ref[...])
    return out, xn, _route(logits)


def _prompt_mixer_kernel(d_ref, u_ref, uprev_ref, gate_ref, x_ref, cw_ref, cb_ref,
                         wpa_ref, wpb_ref, wo_ref, cg_ref, cbn_ref, bpb_ref, g2_ref,
                         wrh_ref, wrl_ref, br_ref,
                         h1_ref, xn_ref, route_ref, upad_ref, ycv_ref):
    j = pl.program_id(1)
    prev = uprev_ref[...]
    upad_ref[0:HALO, :] = jnp.where(j == 0, jnp.zeros_like(prev), prev)
    upad_ref[HALO:, :] = u_ref[...]
    cb = cb_ref[...]
    first = HALO - (CONV_WIDTH - 1)
    for c in range(TC // RC):
        acc = jnp.zeros((RC, CONV_CH), F32)
        for jt in range(CONV_WIDTH):
            r0 = c * RC + first + jt
            acc = acc + cw_ref[jt:jt + 1, :] * upad_ref[r0:r0 + RC, :]
        ycv_ref[c * RC:(c + 1) * RC, :] = acc + cb
    out, xn, route = _mixer_tail(ycv_ref[...], d_ref[...], gate_ref[...], x_ref[...],
                                 wpa_ref, wpb_ref, wo_ref, cg_ref, cbn_ref, bpb_ref,
                                 g2_ref, wrh_ref, wrl_ref, br_ref)
    h1_ref[...] = out
    xn_ref[...] = xn
    route_ref[...] = route


def _mixer_weight_specs(const):
    vec = pl.BlockSpec((1, D_MODEL), const)
    mat = pl.BlockSpec((D_MODEL, D_MODEL), const)
    rt = pl.BlockSpec((D_MODEL, LANES), const)
    return [mat, mat, mat, vec, vec, vec, vec, rt, rt, pl.BlockSpec((1, LANES), const)]


def _prompt_mixer(d3, u3, gate3, x3, cw, cb, tail_w):
    b, s, _ = x3.shape
    blk = lambda bi, ji: (bi, ji, 0)
    prev = lambda bi, ji: (bi, jnp.maximum(ji * (TC // HALO) - 1, 0), 0)
    const = lambda bi, ji: (0, 0)
    return pl.pallas_call(
        _prompt_mixer_kernel,
        grid=(b, s // TC),
        in_specs=[pl.BlockSpec((None, TC, V_W), blk),
                  pl.BlockSpec((None, TC, CONV_CH), blk),
                  pl.BlockSpec((None, HALO, CONV_CH), prev),
                  pl.BlockSpec((None, TC, 2 * D_MODEL), blk),
                  pl.BlockSpec((None, TC, D_MODEL), blk),
                  pl.BlockSpec((HALO, CONV_CH), const),
                  pl.BlockSpec((1, CONV_CH), const)] + _mixer_weight_specs(const),
        out_specs=[pl.BlockSpec((None, TC, D_MODEL), blk),
                   pl.BlockSpec((None, TC, D_MODEL), blk),
                   pl.BlockSpec((None, TC, LANES), blk)],
        out_shape=[jax.ShapeDtypeStruct((b, s, D_MODEL), F32),
                   jax.ShapeDtypeStruct((b, s, D_MODEL), F32),
                   jax.ShapeDtypeStruct((b, s, LANES), F32)],
        scratch_shapes=[pltpu.VMEM((TC + HALO, CONV_CH), F32),
                        pltpu.VMEM((TC, CONV_CH), F32)],
        compiler_params=_cparams(2),
        name="prompt_mixer",
    )(d3, u3, u3, gate3, x3, cw, cb, *tail_w)


def _sample_mixer_kernel(d_ref, xpad_ref, gate_ref, x_ref, cw_ref, cb_ref,
                         wpa_ref, wpb_ref, wo_ref, cg_ref, cbn_ref, bpb_ref, g2_ref,
                         wrh_ref, wrl_ref, br_ref,
                         h1_ref, xn_ref, route_ref, *, dec_seq):
    cb = cb_ref[...]
    parts = []
    for t in range(dec_seq):
        acc = jnp.zeros((xpad_ref.shape[0], CONV_CH), F32)
        for jt in range(CONV_WIDTH):
            acc = acc + cw_ref[jt:jt + 1, :] * xpad_ref[:, t + jt, :]
        parts.append(acc + cb)
    conv_y = jnp.concatenate(parts, axis=0)
    out, xn, route = _mixer_tail(conv_y, d_ref[...], gate_ref[...], x_ref[...],
                                 wpa_ref, wpb_ref, wo_ref, cg_ref, cbn_ref, bpb_ref,
                                 g2_ref, wrh_ref, wrl_ref, br_ref)
    h1_ref[...] = out
    xn_ref[...] = xn
    route_ref[...] = route


def _sample_mixer(d_t, xpad, gate_t, x_t, cw, cb, tail_w, dec_seq):
    n = x_t.shape[0]
    const = lambda i: (0, 0)
    full = lambda shape: pl.BlockSpec(shape, lambda i: (0,) * len(shape))
    return pl.pallas_call(
        functools.partial(_sample_mixer_kernel, dec_seq=dec_seq),
        grid=(1,),
        in_specs=[full((n, V_W)), full(xpad.shape), full((n, 2 * D_MODEL)),
                  full((n, D_MODEL)), full((HALO, CONV_CH)), full((1, CONV_CH))]
                 + _mixer_weight_specs(const),
        out_specs=[full((n, D_MODEL)), full((n, D_MODEL)), full((n, LANES))],
        out_shape=[jax.ShapeDtypeStruct((n, D_MODEL), F32),
                   jax.ShapeDtypeStruct((n, D_MODEL), F32),
                   jax.ShapeDtypeStruct((n, LANES), F32)],
        compiler_params=_cparams(1),
        name="sample_mixer",
    )(d_t, xpad, gate_t, x_t, cw, cb, *tail_w)


def _plan_kernel(route_ref, pos_ref, meta_ref, cnt_ref, off_ref, carry_ref, *, n_meta):
    ph = pl.program_id(0)
    i = pl.program_id(1)
    lane = lax.broadcasted_iota(I32, (TP, LANES), 1)
    r = route_ref[...]
    e0 = r[:, 0:1].astype(I32)
    e1 = r[:, 1:2].astype(I32)
    oh = ((lane == e0) | (lane == e1)).astype(F32)
    tile_cnt = jnp.sum(oh, axis=0, keepdims=True)

    @pl.when((ph == 0) & (i == 0))
    def _():
        cnt_ref[...] = jnp.zeros_like(cnt_ref)

    @pl.when(ph == 0)
    def _():
        cnt_ref[...] += tile_cnt

    @pl.when((ph == 1) & (i == 0))
    def _():
        lane8 = lax.broadcasted_iota(I32, (SUBLANES, LANES), 1)
        cnt = jnp.broadcast_to(cnt_ref[...], (SUBLANES, LANES)).astype(I32)
        padded = ((cnt + (TME - 1)) >> TME_LOG) << TME_LOG
        x = padded
        for s in (1, 2, 4, 8, 16, 32, 64):
            x = x + jnp.where(lane8 >= s, pltpu.roll(x, s, 1), 0)
        off_end = x
        off_ref[...] = (off_end - padded)[0:1, :].astype(F32)
        carry_ref[...] = jnp.zeros_like(carry_ref)
        tile_row = lax.broadcasted_iota(I32, (SUBLANES, n_meta), 1) * TME
        te = jnp.zeros((SUBLANES, n_meta), I32)
        for e in range(N_EXPERTS):
            te = te + (tile_row >= off_end[:, e:e + 1]).astype(I32)
        te = jnp.minimum(te, N_EXPERTS - 1)
        n_valid = jnp.broadcast_to(off_end[:, N_EXPERTS - 1:N_EXPERTS] >> TME_LOG,
                                   (SUBLANES, n_meta))
        sub_m = lax.broadcasted_iota(I32, (SUBLANES, n_meta), 0)
        meta_ref[...] = jnp.where(sub_m == 0, te, n_valid)

    @pl.when(ph == 1)
    def _():
        rr = lax.broadcasted_iota(I32, (TP, TP), 0)
        cc = lax.broadcasted_iota(I32, (TP, TP), 1)
        tril = (cc <= rr).astype(BF16)
        cum = _dot(tril, oh.astype(BF16))
        slot = off_ref[...] + carry_ref[...] + cum - oh
        p0 = jnp.sum(jnp.where(lane == e0, slot, 0.0), axis=-1, keepdims=True)
        p1 = jnp.sum(jnp.where(lane == e1, slot, 0.0), axis=-1, keepdims=True)
        slab = jnp.where(lane == 0, p0, jnp.where(lane == 1, p1, 0.0))
        pos_ref[...] = slab.T[0:SUBLANES, :].astype(I32)
        carry_ref[...] += tile_cnt


def _plan(route2d, n_meta):
    n = route2d.shape[0]
    return pl.pallas_call(
        functools.partial(_plan_kernel, n_meta=n_meta),
        grid=(2, n // TP),
        in_specs=[pl.BlockSpec((TP, LANES), lambda ph, i: (i, 0))],
        out_specs=[pl.BlockSpec((SUBLANES, TP), lambda ph, i: (0, i * ph)),
                   pl.BlockSpec((SUBLANES, n_meta), lambda ph, i: (0, 0))],
        out_shape=[jax.ShapeDtypeStruct((SUBLANES, n), I32),
                   jax.ShapeDtypeStruct((SUBLANES, n_meta), I32)],
        scratch_shapes=[pltpu.VMEM((1, LANES), F32), pltpu.VMEM((1, LANES), F32),
                        pltpu.VMEM((1, LANES), F32)],
        compiler_params=_cparams(2),
        name="moe_plan",
    )(route2d)


def _scatter_kernel(pos_ref, xn_hbm, xs_in_hbm, xs_hbm, sem):
    del xs_in_hbm
    base = pl.program_id(0) * TS

    def row_copy(t, p):
        return pltpu.make_async_copy(xn_hbm.at[pl.ds(t, 1)], xs_hbm.at[pl.ds(p, 1)], sem)

    def start(i, c):
        row_copy(base + i, pos_ref[0, i]).start()
        row_copy(base + i, pos_ref[1, i]).start()
        return c

    def wait(i, c):
        row_copy(0, 0).wait()
        row_copy(0, 0).wait()
        return c

    lax.fori_loop(0, TS, start, 0)
    lax.fori_loop(0, TS, wait, 0)


def _scatter(pos, xn2d, xs_zero):
    n = xn2d.shape[0]
    return pl.pallas_call(
        _scatter_kernel,
        grid=(n // TS,),
        in_specs=[pl.BlockSpec((SUBLANES, TS), lambda i: (0, i), memory_space=pltpu.SMEM),
                  pl.BlockSpec(memory_space=pl.ANY),
                  pl.BlockSpec(memory_space=pl.ANY)],
        out_specs=pl.BlockSpec(memory_space=pl.ANY),
        out_shape=jax.ShapeDtypeStruct(xs_zero.shape, xs_zero.dtype),
        scratch_shapes=[pltpu.SemaphoreType.DMA],
        input_output_aliases={2: 0},
        compiler_params=_cparams(1),
        name="moe_scatter",
    )(pos, xn2d, xs_zero)


def _gmm_kernel(meta_ref, x_ref, wg_ref, wu_ref, wd_ref, y_ref):
    t = pl.program_id(0)
    n_valid = meta_ref[1, 0]

    @pl.when(t < n_valid)
    def _():
        x = x_ref[...].astype(BF16)
        a = _dot(x, wg_ref[...])
        b = _dot(x, wu_ref[...])
        hh = (a * _sigmoid(a)) * b
        y_ref[...] = _dot(hh.astype(BF16), wd_ref[...])

    @pl.when(t >= n_valid)
    def _():
        y_ref[...] = jnp.zeros_like(y_ref)


def _gmm(meta, xs, wg, wu, wd):
    n_tiles = xs.shape[0] // TME
    w_idx = lambda t, meta: (meta[0, t], 0, 0)
    grid_spec = pltpu.PrefetchScalarGridSpec(
        num_scalar_prefetch=1,
        grid=(n_tiles,),
        in_specs=[pl.BlockSpec((TME, D_MODEL), lambda t, meta: (t, 0)),
                  pl.BlockSpec((None, D_MODEL, EXPERT_FF), w_idx),
                  pl.BlockSpec((None, D_MODEL, EXPERT_FF), w_idx),
                  pl.BlockSpec((None, EXPERT_FF, D_MODEL), w_idx)],
        out_specs=pl.BlockSpec((TME, D_MODEL), lambda t, meta: (t, 0)),
    )
    return pl.pallas_call(
        _gmm_kernel,
        grid_spec=grid_spec,
        out_shape=jax.ShapeDtypeStruct(xs.shape, F32),
        compiler_params=_cparams(1),
        name="moe_gmm",
    )(meta, xs, wg, wu, wd)


def _combine_kernel(pos_ref, route_ref, h1_ref, gf_ref, ys_hbm, y_ref, buf_ref, sem):
    def row_copy(k, i, p):
        return pltpu.make_async_copy(ys_hbm.at[pl.ds(p, 1)], buf_ref.at[k, pl.ds(i, 1)], sem)

    def start(i, c):
        row_copy(0, i, pos_ref[0, i]).start()
        row_copy(1, i, pos_ref[1, i]).start()
        return c

    def wait(i, c):
        row_copy(0, 0, 0).wait()
        row_copy(1, 0, 0).wait()
        return c

    lax.fori_loop(0, TF, start, 0)
    lax.fori_loop(0, TF, wait, 0)
    r = route_ref[...]
    out = h1_ref[...] + r[:, 2:3] * buf_ref[0] + r[:, 3:4] * buf_ref[1]
    ms = jnp.mean(out * out, axis=-1, keepdims=True)
    y_ref[...] = out * lax.rsqrt(ms + EPS) * gf_ref[...]


def _combine(pos, route2d, h1_2d, gf, ys):
    n = h1_2d.shape[0]
    row = lambda i: (i, 0)
    return pl.pallas_call(
        _combine_kernel,
        grid=(n // TF,),
        in_specs=[pl.BlockSpec((SUBLANES, TF), lambda i: (0, i), memory_space=pltpu.SMEM),
                  pl.BlockSpec((TF, LANES), row),
                  pl.BlockSpec((TF, D_MODEL), row),
                  pl.BlockSpec((1, D_MODEL), lambda i: (0, 0)),
                  pl.BlockSpec(memory_space=pl.ANY)],
        out_specs=pl.BlockSpec((TF, D_MODEL), row),
        out_shape=jax.ShapeDtypeStruct((n, D_MODEL), F32),
        scratch_shapes=[pltpu.VMEM((2, TF, D_MODEL), F32), pltpu.SemaphoreType.DMA],
        compiler_params=_cparams(1),
        name="moe_combine",
    )(pos, route2d, h1_2d, gf, ys)


def _sample_moe_kernel(xn_ref, route_ref, h1_ref, gf_ref, wg_ref, wu_ref, wd_ref,
                       y_ref, acc_ref):
    e = pl.program_id(0)

    @pl.when(e == 0)
    def _():
        acc_ref[...] = jnp.zeros_like(acc_ref)

    r = route_ref[...]
    ef = e.astype(F32)
    gate = jnp.where(r[:, 0:1] == ef, r[:, 2:3], 0.0) + jnp.where(r[:, 1:2] == ef, r[:, 3:4], 0.0)
    x = xn_ref[...].astype(BF16)
    a = _dot(x, wg_ref[...])
    b = _dot(x, wu_ref[...])
    hh = (a * _sigmoid(a)) * b
    acc_ref[...] += gate * _dot(hh.astype(BF16), wd_ref[...])

    @pl.when(e == N_EXPERTS - 1)
    def _():
        out = h1_ref[...] + acc_ref[...]
        ms = jnp.mean(out * out, axis=-1, keepdims=True)
        y_ref[...] = out * lax.rsqrt(ms + EPS) * gf_ref[...]


def _sample_moe(xn, route, h1, gf, wg, wu, wd):
    n = xn.shape[0]
    const = lambda e: (0, 0)
    w_idx = lambda e: (e, 0, 0)
    return pl.pallas_call(
        _sample_moe_kernel,
        grid=(N_EXPERTS,),
        in_specs=[pl.BlockSpec((n, D_MODEL), const),
                  pl.BlockSpec((n, LANES), const),
                  pl.BlockSpec((n, D_MODEL), const),
                  pl.BlockSpec((1, D_MODEL), const),
                  pl.BlockSpec((None, D_MODEL, EXPERT_FF), w_idx),
                  pl.BlockSpec((None, D_MODEL, EXPERT_FF), w_idx),
                  pl.BlockSpec((None, EXPERT_FF, D_MODEL), w_idx)],
        out_specs=pl.BlockSpec((n, D_MODEL), const),
        out_shape=jax.ShapeDtypeStruct((n, D_MODEL), F32),
        scratch_shapes=[pltpu.VMEM((n, D_MODEL), F32)],
        compiler_params=_cparams(1),
        name="sample_moe",
    )(xn, route, h1, gf, wg, wu, wd)


def kernel(x_prompt, x_sample, cache_k, cache_v, state_conv, page_table, norm1_g, w_in, b_gate,
           lambda_q1, lambda_k1, lambda_q2, lambda_k2, attn_norm_g, w_pa, conv_w, conv_b,
           conv_norm_g, conv_norm_b, w_pb, b_pb, w_o, norm2_g, w_router_group, b_router_group,
           w_router_expert, b_router_expert, w_exp_gate, w_exp_up, w_exp_down, final_norm_g):
    depth = w_in.shape[0]
    assert depth == 1, "kernels are written for a single layer"
    b, s, _ = x_prompt.shape
    bd, dec_seq, _ = x_sample.shape
    n_pages = page_table.shape[1]
    past_len = n_pages * PAGE_SIZE
    n_p = b * s
    n_s = bd * dec_seq
    assert s % TM_IN == 0 and s % TC == 0 and s % TQ == 0 and n_p % TS == 0
    assert n_pages % PG == 0 and dec_seq <= SUBLANES
    l = 0
    lam_init = 0.8 - 0.6 * math.exp(-0.3 * l)

    row = lambda a: a.reshape(1, -1)
    g1 = row(norm1_g[l])
    w_in_bf = w_in[l].astype(BF16)
    bg = row(b_gate[l])
    lams = (row(lambda_q1[l]), row(lambda_k1[l]), row(lambda_q2[l]), row(lambda_k2[l]))
    g_attn = row(attn_norm_g[l])
    cw = jnp.pad(conv_w[l], ((0, HALO - CONV_WIDTH), (0, 0)))
    cb = row(conv_b[l])
    w_r = jnp.pad(jnp.concatenate([w_router_group[l], w_router_expert[l]], axis=1),
                  ((0, 0), (0, LANES - N_GROUPS - N_EXPERTS)))
    b_r = jnp.pad(jnp.concatenate([b_router_group[l], b_router_expert[l]]),
                  (0, LANES - N_GROUPS - N_EXPERTS)).reshape(1, LANES)
    w_r_hi = w_r.astype(BF16)
    w_r_lo = (w_r - w_r_hi.astype(F32)).astype(BF16)
    tail_w = (w_pa[l].astype(BF16), w_pb[l].astype(BF16), w_o[l].astype(BF16),
              row(conv_norm_g[l]), row(conv_norm_b[l]), row(b_pb[l]), row(norm2_g[l]),
              w_r_hi, w_r_lo, b_r)
    wg = w_exp_gate[l].astype(BF16)
    wu = w_exp_up[l].astype(BF16)
    wd = w_exp_down[l].astype(BF16)
    gf = row(final_norm_g)

    cos_p, sa_p, sb_p = _rope_tables(jnp.arange(s, dtype=F32))
    q_p, k_p, v_p, u_p, gate_p = _inproj(x_prompt.reshape(n_p, D_MODEL), g1, w_in_bf, bg,
                                         cos_p, sa_p, sb_p, TM_IN, s // TM_IN)
    k_p3 = k_p.reshape(b, s, QK_W)
    v_p3 = v_p.reshape(b, s, V_W)
    u_p3 = u_p.reshape(b, s, CONV_CH)
    d_p = _prompt_attn(lams, g_attn, q_p.reshape(b, s, QK_W), k_p3, v_p3, lam_init)
    h1_p, xn_p, route_p = _prompt_mixer(d_p, u_p3, gate_p.reshape(b, s, 2 * D_MODEL),
                                        x_prompt, cw, cb, tail_w)
    route_p2 = route_p.reshape(n_p, LANES)
    n_slots = 2 * n_p + N_EXPERTS * TME
    n_meta = pl.cdiv(n_slots // TME, LANES) * LANES
    pos_p, meta_p = _plan(route_p2, n_meta)
    xs = _scatter(pos_p, xn_p.reshape(n_p, D_MODEL), jnp.zeros((n_slots, D_MODEL), F32))
    ys = _gmm(meta_p, xs, wg, wu, wd)
    y_prompt = _combine(pos_p, route_p2, h1_p.reshape(n_p, D_MODEL), gf, ys)
    y_prompt = y_prompt.reshape(b, s, D_MODEL)

    pos_s = past_len + jnp.arange(dec_seq, dtype=F32)
    cos_s, sa_s, sb_s = (jnp.tile(t, (bd, 1)) for t in _rope_tables(pos_s))
    q_s, k_s, v_s, u_s, gate_s = _inproj(x_sample.reshape(n_s, D_MODEL), g1, w_in_bf, bg,
                                         cos_s, sa_s, sb_s, n_s, 1)
    q4 = q_s.reshape(bd, dec_seq, N_HEADS, 2, HEAD_DIM)
    qt = jnp.transpose(q4, (0, 3, 1, 2, 4))
    mask = (jnp.arange(N_HEADS)[None, :, None, None] == jnp.arange(N_HEADS)[None, None, :, None]) \
        & (jnp.arange(2)[:, None, None, None] == jnp.arange(2)[None, None, None, :])
    qbd = jnp.where(mask[None, :, None, :, :, :, None], qt[:, :, :, :, None, None, :],
                    jnp.zeros((), BF16))
    qbd = qbd.reshape(bd, 2 * dec_seq * N_HEADS, QK_W)
    n_pool = cache_k.shape[1]
    pad8 = lambda a: jnp.pad(a.reshape(bd, dec_seq, -1), ((0, 0), (0, SUBLANES - dec_seq), (0, 0)))
    d_s = _sample_attn(page_table, lams, g_attn, qbd, pad8(k_s), pad8(v_s),
                       cache_k[l].reshape(n_pool, PAGE_SIZE, QK_W),
                       cache_v[l].reshape(n_pool, PAGE_SIZE, V_W), dec_seq, lam_init)
    d_s = d_s[:, :dec_seq, :]
    tmaj = lambda a: jnp.transpose(a.reshape(bd, dec_seq, -1), (1, 0, 2)).reshape(n_s, -1)
    xpad = jnp.concatenate([state_conv[l], u_s.reshape(bd, dec_seq, CONV_CH)], axis=1)
    h1_s, xn_s, route_s = _sample_mixer(tmaj(d_s), xpad, tmaj(gate_s), tmaj(x_sample),
                                        cw, cb, tail_w, dec_seq)
    y_s = _sample_moe(xn_s, route_s, h1_s, gf, wg, wu, wd)
    y_sample = jnp.transpose(y_s.reshape(dec_seq, bd, D_MODEL), (1, 0, 2))

    k_prompt = k_p.reshape(1, b, s, N_HEADS, 2 * HEAD_DIM)
    v_prompt = v_p.reshape(1, b, s, N_HEADS, V_DIM)
    conv_prompt = u_p3[:, s - (CONV_WIDTH - 1):, :][None]
    k_sample = k_s.reshape(1, bd, dec_seq, N_HEADS, 2 * HEAD_DIM)
    v_sample = v_s.reshape(1, bd, dec_seq, N_HEADS, V_DIM)
    conv_sample = xpad[:, dec_seq:, :][None]
    return (y_prompt, y_sample, k_prompt, v_prompt, conv_prompt, k_sample, v_sample, conv_sample)
```

```python
import functools
import math

import jax
import jax.numpy as jnp
from jax import lax
from jax.experimental import pallas as pl
from jax.experimental.pallas import tpu as pltpu

F32 = jnp.float32
BF16 = jnp.bfloat16
I32 = jnp.int32

D_MODEL = 1024
N_HEADS = 8
HEAD_DIM = 64
V_DIM = 2 * HEAD_DIM
QK_W = N_HEADS * 2 * HEAD_DIM
V_W = N_HEADS * V_DIM
CONV_CH = D_MODEL
CONV_WIDTH = 31
N_GROUPS = 4
EXPERTS_PER_GROUP = 8
N_EXPERTS = N_GROUPS * EXPERTS_PER_GROUP
EXPERT_FF = D_MODEL // 2
ROPE_THETA = 10000.0
PAGE_SIZE = 128
EPS = 1e-6
QK_SCALE = HEAD_DIM ** -0.5

LANES = 128
SUBLANES = 8
VMEM_LIMIT = 56 * 1024 * 1024

TM_IN = 512
TQ = 256
TC = 256
HALO = 32
RC = SUBLANES
TME = 256
TME_LOG = 8
TP = 256
TS = 512
TF = 256
DMA_UNROLL = 8
PG = 8


def _cparams(n_axes):
    return pltpu.CompilerParams(
        dimension_semantics=("arbitrary",) * n_axes, vmem_limit_bytes=VMEM_LIMIT)


def _sigmoid(x):
    return 1.0 / (1.0 + jnp.exp(-x))


def _dot(a, b):
    return jnp.dot(a, b, preferred_element_type=F32)


def _dot_nt(a, b):
    return lax.dot_general(a, b, (((1,), (1,)), ((), ())), preferred_element_type=F32)


def _inproj_kernel(x_ref, g1_ref, w_ref, bg_ref, cos_ref, sa_ref, sb_ref,
                   q_ref, k_ref, v_ref, u_ref, gate_ref):
    x = x_ref[...]
    ms = jnp.mean(x * x, axis=-1, keepdims=True)
    h = (x * lax.rsqrt(ms + EPS) * g1_ref[...]).astype(BF16)
    cos = cos_ref[...]
    sa = sa_ref[...]
    sb = sb_ref[...]

    def rot(xs):
        return xs * cos + pltpu.roll(xs, 96, 1) * sa + pltpu.roll(xs, 32, 1) * sb

    pq = _dot(h, w_ref[:, 0:QK_W])
    for hh in range(N_HEADS):
        sl = slice(hh * LANES, (hh + 1) * LANES)
        q_ref[:, sl] = (rot(pq[:, sl]) * QK_SCALE).astype(BF16)
    pk = _dot(h, w_ref[:, QK_W:2 * QK_W])
    for hh in range(N_HEADS):
        sl = slice(hh * LANES, (hh + 1) * LANES)
        k_ref[:, sl] = rot(pk[:, sl])
    o_v, o_c = 2 * QK_W, 2 * QK_W + V_W
    v_ref[...] = _dot(h, w_ref[:, o_v:o_c])
    pa = _dot(h, w_ref[:, o_c:o_c + CONV_CH])
    pb = _dot(h, w_ref[:, o_c + CONV_CH:o_c + 2 * CONV_CH])
    u_ref[...] = pa * _sigmoid(pb)
    o_g = o_c + 2 * CONV_CH
    pg = _dot(h, w_ref[:, o_g:o_g + 2 * D_MODEL]) + bg_ref[...]
    gate_ref[...] = _sigmoid(pg).astype(BF16)


def _inproj(x2d, g1, w_in_bf, b_gate, cos_t, sa_t, sb_t, tm, table_blocks):
    n = x2d.shape[0]
    in_w = w_in_bf.shape[1]
    row = lambda i: (i, 0)
    const = lambda i: (0, 0)
    tab = lambda i: (i % table_blocks, 0)
    return pl.pallas_call(
        _inproj_kernel,
        grid=(n // tm,),
        in_specs=[
            pl.BlockSpec((tm, D_MODEL), row),
            pl.BlockSpec((1, D_MODEL), const),
            pl.BlockSpec((D_MODEL, in_w), const, pipeline_mode=pl.Buffered(1)),
            pl.BlockSpec((1, 2 * D_MODEL), const),
            pl.BlockSpec((tm, LANES), tab),
            pl.BlockSpec((tm, LANES), tab),
            pl.BlockSpec((tm, LANES), tab),
        ],
        out_specs=[
            pl.BlockSpec((tm, QK_W), row),
            pl.BlockSpec((tm, QK_W), row),
            pl.BlockSpec((tm, V_W), row),
            pl.BlockSpec((tm, CONV_CH), row),
            pl.BlockSpec((tm, 2 * D_MODEL), row),
        ],
        out_shape=[
            jax.ShapeDtypeStruct((n, QK_W), BF16),
            jax.ShapeDtypeStruct((n, QK_W), F32),
            jax.ShapeDtypeStruct((n, V_W), F32),
            jax.ShapeDtypeStruct((n, CONV_CH), F32),
            jax.ShapeDtypeStruct((n, 2 * D_MODEL), BF16),
        ],
        compiler_params=_cparams(1),
        name="inproj",
    )(x2d, g1, w_in_bf, b_gate, cos_t, sa_t, sb_t)


def _rope_tables(pos):
    half = HEAD_DIM // 2
    inv = 1.0 / (ROPE_THETA ** (jnp.arange(half, dtype=F32) * 2.0 / HEAD_DIM))
    ang = pos[:, None] * inv[None, :]
    cos = jnp.tile(jnp.cos(ang), (1, LANES // half))
    sin = jnp.tile(jnp.sin(ang), (1, LANES // half))
    second_half = ((jnp.arange(LANES) % HEAD_DIM) // half)[None, :] == 1
    sa = jnp.where(second_half, 0.0, -sin)
    sb = jnp.where(second_half, sin, 0.0)
    return cos, sa, sb


def _diff_lambda_vec(lq1_ref, lk1_ref, lq2_ref, lk2_ref, lam_init):
    e1 = jnp.exp(jnp.sum(lq1_ref[...] * lk1_ref[...], axis=-1, keepdims=True))
    e2 = jnp.exp(jnp.sum(lq2_ref[...] * lk2_ref[...], axis=-1, keepdims=True))
    return e1 - e2 + lam_init


def _head_norm(d, g, lam_init):
    ms = jnp.mean(d * d, axis=-1, keepdims=True)
    return d * lax.rsqrt(ms + EPS) * g * (1.0 - lam_init)


def _prompt_attn_kernel(lq1_ref, lk1_ref, lq2_ref, lk2_ref, g_ref, mask_ref, q_ref, k_ref, v_ref,
                        o_ref, kb_ref, vb_ref, *, seq, lam_init):
    kb_ref[...] = k_ref[...].astype(BF16)
    vb_ref[...] = v_ref[...].astype(BF16)
    lam = _diff_lambda_vec(lq1_ref, lk1_ref, lq2_ref, lk2_ref, lam_init)
    g = g_ref[...]
    row = lax.broadcasted_iota(I32, (TQ, TQ), 0)
    col = lax.broadcasted_iota(I32, (TQ, TQ), 1)
    causal = col <= row
    for i in range(seq // TQ):
        q = q_ref[i * TQ:(i + 1) * TQ, :]
        lo = i * TQ
        k_diag = kb_ref[lo:lo + TQ, :]
        v_diag = vb_ref[lo:lo + TQ, :]
        outs = []
        for m in range(2):
            qm = q * mask_ref[m]
            s_d = jnp.where(causal, _dot_nt(qm, k_diag), -jnp.inf)
            mx = jnp.max(s_d, axis=-1, keepdims=True)
            if i > 0:
                s_o = _dot_nt(qm, kb_ref[0:lo, :])
                mx = jnp.maximum(mx, jnp.max(s_o, axis=-1, keepdims=True))
            p_d = jnp.exp(s_d - mx)
            l = jnp.sum(p_d, axis=-1, keepdims=True)
            o = _dot(p_d.astype(BF16), v_diag)
            if i > 0:
                p_o = jnp.exp(s_o - mx)
                l = l + jnp.sum(p_o, axis=-1, keepdims=True)
                o = o + _dot(p_o.astype(BF16), vb_ref[0:lo, :])
            outs.append(o / l)
        d = outs[0] - lam * outs[1]
        o_ref[lo:lo + TQ, :] = _head_norm(d, g, lam_init).astype(BF16)


def _prompt_attn(lams, g, q3, k3, v3, lam_init):
    b, s, _ = q3.shape
    blk = lambda bi, hi: (bi, 0, hi)
    const = lambda bi, hi: (0, 0)
    lam_spec = pl.BlockSpec((1, HEAD_DIM), const)
    lane_map = jnp.arange(LANES) // HEAD_DIM
    map_mask = jnp.broadcast_to((lane_map[None, None, :] == jnp.arange(2)[:, None, None]),
                                (2, TQ, LANES)).astype(BF16)
    return pl.pallas_call(
        functools.partial(_prompt_attn_kernel, seq=s, lam_init=lam_init),
        grid=(b, N_HEADS),
        in_specs=[lam_spec, lam_spec, lam_spec, lam_spec,
                  pl.BlockSpec((1, V_DIM), const),
                  pl.BlockSpec((2, TQ, LANES), lambda bi, hi: (0, 0, 0)),
                  pl.BlockSpec((None, s, LANES), blk),
                  pl.BlockSpec((None, s, LANES), blk),
                  pl.BlockSpec((None, s, LANES), blk)],
        out_specs=pl.BlockSpec((None, s, LANES), blk),
        out_shape=jax.ShapeDtypeStruct((b, s, V_W), BF16),
        scratch_shapes=[pltpu.VMEM((s, LANES), BF16), pltpu.VMEM((s, LANES), BF16)],
        compiler_params=_cparams(2),
        name="prompt_attn",
    )(*lams, g, map_mask, q3, k3, v3)


def _sample_attn_kernel(pt_ref, lq1_ref, lk1_ref, lq2_ref, lk2_ref, g_ref, qbd_ref,
                        knew_ref, vnew_ref, *rest, n_steps, dec_seq, lam_init):
    k_refs = rest[:PG]
    v_refs = rest[PG:2 * PG]
    o_ref = rest[2 * PG]
    qf_ref, m_ref, l_ref, acc_ref = rest[2 * PG + 1:]
    step = pl.program_id(1)
    n_rows = 2 * dec_seq * N_HEADS

    @pl.when(step == 0)
    def _():
        qf_ref[...] = qbd_ref[...].astype(F32)
        m_ref[...] = jnp.full(m_ref.shape, -jnp.inf, F32)
        l_ref[...] = jnp.zeros(l_ref.shape, F32)
        acc_ref[...] = jnp.zeros(acc_ref.shape, F32)

    qf = qf_ref[...]

    def update(s_parts, v_parts):
        m_old = m_ref[...]
        mx = m_old
        for s in s_parts:
            mx = jnp.maximum(mx, jnp.max(s, axis=-1, keepdims=True))
        corr = jnp.exp(m_old - mx)
        l = l_ref[...] * corr
        acc = acc_ref[...] * corr
        for s, v in zip(s_parts, v_parts):
            p = jnp.exp(s - mx)
            l = l + jnp.sum(p, axis=-1, keepdims=True)
            acc = acc + _dot(p, v)
        m_ref[...] = mx
        l_ref[...] = l
        acc_ref[...] = acc

    update([_dot_nt(qf, k_refs[i][...]) for i in range(PG)],
           [v_refs[i][...] for i in range(PG)])

    @pl.when(step == n_steps - 1)
    def _():
        k_new = knew_ref[...]
        v_new = vnew_ref[...]
        s_new = _dot_nt(qf, k_new)
        r = lax.broadcasted_iota(I32, (n_rows, SUBLANES), 0)
        j = lax.broadcasted_iota(I32, (n_rows, SUBLANES), 1)
        qi = (r % (dec_seq * N_HEADS)) // N_HEADS
        s_new = jnp.where(j <= qi, s_new, -jnp.inf)
        update([s_new], [v_new])
        o = acc_ref[...] / l_ref[...]
        lam = _diff_lambda_vec(lq1_ref, lk1_ref, lq2_ref, lk2_ref, lam_init)
        sub = lax.broadcasted_iota(I32, (N_HEADS, D_MODEL), 0)
        lane_head = lax.broadcasted_iota(I32, (N_HEADS, D_MODEL), 1) // V_DIM
        own_head = sub == lane_head
        g = g_ref[...]
        d_rows = []
        for t in range(dec_seq):
            rows = []
            for m in range(2):
                r0 = m * dec_seq * N_HEADS + t * N_HEADS
                blk = jnp.where(own_head, o[r0:r0 + N_HEADS, :], 0.0)
                rows.append(jnp.sum(blk, axis=0, keepdims=True))
            d_rows.append(rows[0] - lam * rows[1])
        d_rows += [jnp.zeros((1, V_W), F32)] * (SUBLANES - dec_seq)
        d = jnp.concatenate(d_rows, axis=0)
        o_ref[...] = jnp.concatenate(
            [_head_norm(d[:, hh * V_DIM:(hh + 1) * V_DIM], g, lam_init)
             for hh in range(N_HEADS)], axis=1)


def _sample_attn(page_table, lams, g, qbd, k_new3, v_new3, cache_k3, cache_v3, dec_seq,
                 lam_init):
    bd, n_pages = page_table.shape
    n_rows = qbd.shape[1]
    n_steps = n_pages // PG
    pt_flat = page_table.reshape(-1)
    const = lambda bi, si, pt: (0, 0)
    per_b = lambda bi, si, pt: (bi, 0, 0)
    lam_spec = pl.BlockSpec((1, HEAD_DIM), const)

    def page_spec(i):
        return pl.BlockSpec(
            (None, PAGE_SIZE, D_MODEL),
            lambda bi, si, pt, i=i: (pt[bi * n_pages + si * PG + i], 0, 0))

    grid_spec = pltpu.PrefetchScalarGridSpec(
        num_scalar_prefetch=1,
        grid=(bd, n_steps),
        in_specs=[lam_spec, lam_spec, lam_spec, lam_spec,
                  pl.BlockSpec((1, V_DIM), const),
                  pl.BlockSpec((None, n_rows, D_MODEL), per_b),
                  pl.BlockSpec((None, SUBLANES, D_MODEL), per_b),
                  pl.BlockSpec((None, SUBLANES, D_MODEL), per_b)]
                 + [page_spec(i) for i in range(PG)]
                 + [page_spec(i) for i in range(PG)],
        out_specs=pl.BlockSpec((None, SUBLANES, V_W), per_b),
        scratch_shapes=[pltpu.VMEM((n_rows, D_MODEL), F32),
                        pltpu.VMEM((n_rows, 1), F32),
                        pltpu.VMEM((n_rows, 1), F32),
                        pltpu.VMEM((n_rows, V_W), F32)],
    )
    return pl.pallas_call(
        functools.partial(_sample_attn_kernel, n_steps=n_steps, dec_seq=dec_seq,
                          lam_init=lam_init),
        grid_spec=grid_spec,
        out_shape=jax.ShapeDtypeStruct((bd, SUBLANES, V_W), F32),
        compiler_params=_cparams(2),
        name="sample_attn",
    )(pt_flat, *lams, g, qbd, k_new3, v_new3, *([cache_k3] * PG), *([cache_v3] * PG))


def _route(logits):
    shape = logits.shape
    lane = lax.broadcasted_iota(I32, shape, 1)
    lane_f = lane.astype(F32)
    none = float(LANES)
    neg = -jnp.inf
    is_group = lane < N_GROUPS
    gl = jnp.where(is_group, logits, neg)
    gmax = jnp.max(gl, axis=-1, keepdims=True)
    gidx = jnp.min(jnp.where(gl == gmax, lane_f, none), axis=-1, keepdims=True).astype(I32)
    gsum = jnp.sum(jnp.where(is_group, jnp.exp(logits - gmax), 0.0), axis=-1, keepdims=True)
    g_w = 1.0 / gsum
    in_group = ((lane >= N_GROUPS) & (lane < N_GROUPS + N_EXPERTS)
                & (((lane - N_GROUPS) >> 3) == gidx))
    el = jnp.where(in_group, logits, neg)
    v0 = jnp.max(el, axis=-1, keepdims=True)
    i0 = jnp.min(jnp.where(el == v0, lane_f, none), axis=-1, keepdims=True)
    el2 = jnp.where(lane_f == i0, neg, el)
    v1 = jnp.max(el2, axis=-1, keepdims=True)
    i1 = jnp.min(jnp.where(el2 == v1, lane_f, none), axis=-1, keepdims=True)
    e = jnp.exp(v1 - v0)
    den = 1.0 + e
    w0 = g_w / den
    w1 = g_w * e / den
    eid0 = i0 - float(N_GROUPS)
    eid1 = i1 - float(N_GROUPS)
    return jnp.where(lane == 0, eid0,
                     jnp.where(lane == 1, eid1,
                               jnp.where(lane == 2, w0,
                                         jnp.where(lane == 3, w1, 0.0))))


def _mixer_tail(conv_y, d, gates, x, wpa_ref, wpb_ref, wo_ref, cg_ref, cbn_ref, bpb_ref,
                g2_ref, wrh_ref, wrl_ref, br_ref):
    mu = jnp.mean(conv_y, axis=-1, keepdims=True)
    yc = conv_y - mu
    var = jnp.mean(yc * yc, axis=-1, keepdims=True)
    yn = yc * lax.rsqrt(var + EPS) * cg_ref[...] + cbn_ref[...]
    act = yn * _sigmoid(yn)
    y_b = _dot(act.astype(BF16), wpb_ref[...]) + bpb_ref[...]
    y_a = _dot(d.astype(BF16), wpa_ref[...])
    merged = gates[:, :D_MODEL].astype(F32) * y_a + gates[:, D_MODEL:].astype(F32) * y_b
    out = x + _dot(merged.astype(BF16), wo_ref[...])
    ms = jnp.mean(out * out, axis=-1, keepdims=True)
    xn = out * lax.rsqrt(ms + EPS) * g2_ref[...]
    xh = xn.astype(BF16)
    xl = (xn - xh.astype(F32)).astype(BF16)
    logits = (_dot(xh, wrh_ref[...]) + _dot(xl, wrh_ref[...]) + _dot(xh, wrl_ref[...])
              + br_ref[...])
    return out, xn, _route(logits)


def _prompt_mixer_kernel(d_ref, u_ref, uprev_ref, gate_ref, x_ref, cw_ref, cb_ref,
                         wpa_ref, wpb_ref, wo_ref, cg_ref, cbn_ref, bpb_ref, g2_ref,
                         wrh_ref, wrl_ref, br_ref,
                         h1_ref, xn_ref, route_ref, ush_ref, ycv_ref):
    j = pl.program_id(1)
    prev = uprev_ref[...]
    rows = TC + HALO
    ush_ref[0, 0:HALO, :] = jnp.where(j == 0, jnp.zeros_like(prev), prev)
    ush_ref[0, HALO:, :] = u_ref[...]
    window = ush_ref[0]
    for r in range(1, SUBLANES):
        ush_ref[r] = pltpu.roll(window, rows - r, 0)
    cb = cb_ref[...]
    first = HALO - (CONV_WIDTH - 1)
    for c in range(TC // RC):
        acc = jnp.zeros((RC, CONV_CH), F32)
        for jt in range(CONV_WIDTH):
            a, r = divmod(first + jt, SUBLANES)
            r0 = c * RC + a * SUBLANES
            acc = acc + cw_ref[jt] * ush_ref[r, r0:r0 + RC, :]
        ycv_ref[c * RC:(c + 1) * RC, :] = acc + cb
    out, xn, route = _mixer_tail(ycv_ref[...], d_ref[...], gate_ref[...], x_ref[...],
                                 wpa_ref, wpb_ref, wo_ref, cg_ref, cbn_ref, bpb_ref,
                                 g2_ref, wrh_ref, wrl_ref, br_ref)
    h1_ref[...] = out
    xn_ref[...] = xn
    route_ref[...] = route


def _mixer_weight_specs(const):
    vec = pl.BlockSpec((1, D_MODEL), const)
    mat = pl.BlockSpec((D_MODEL, D_MODEL), const)
    rt = pl.BlockSpec((D_MODEL, LANES), const)
    return [mat, mat, mat, vec, vec, vec, vec, rt, rt, pl.BlockSpec((1, LANES), const)]


def _prompt_mixer(d3, u3, gate3, x3, cw, cb, tail_w):
    b, s, _ = x3.shape
    blk = lambda bi, ji: (bi, ji, 0)
    prev = lambda bi, ji: (bi, jnp.maximum(ji * (TC // HALO) - 1, 0), 0)
    const = lambda bi, ji: (0, 0)
    return pl.pallas_call(
        _prompt_mixer_kernel,
        grid=(b, s // TC),
        in_specs=[pl.BlockSpec((None, TC, V_W), blk),
                  pl.BlockSpec((None, TC, CONV_CH), blk),
                  pl.BlockSpec((None, HALO, CONV_CH), prev),
                  pl.BlockSpec((None, TC, 2 * D_MODEL), blk),
                  pl.BlockSpec((None, TC, D_MODEL), blk),
                  pl.BlockSpec((HALO, SUBLANES, CONV_CH), lambda bi, ji: (0, 0, 0)),
                  pl.BlockSpec((1, CONV_CH), const)] + _mixer_weight_specs(const),
        out_specs=[pl.BlockSpec((None, TC, D_MODEL), blk),
                   pl.BlockSpec((None, TC, D_MODEL), blk),
                   pl.BlockSpec((None, TC, LANES), blk)],
        out_shape=[jax.ShapeDtypeStruct((b, s, D_MODEL), F32),
                   jax.ShapeDtypeStruct((b, s, D_MODEL), F32),
                   jax.ShapeDtypeStruct((b, s, LANES), F32)],
        scratch_shapes=[pltpu.VMEM((SUBLANES, TC + HALO, CONV_CH), F32),
                        pltpu.VMEM((TC, CONV_CH), F32)],
        compiler_params=_cparams(2),
        name="prompt_mixer",
    )(d3, u3, u3, gate3, x3, cw, cb, *tail_w)


def _sample_mixer_kernel(d_ref, xpad_ref, gate_ref, x_ref, cw_ref, cb_ref,
                         wpa_ref, wpb_ref, wo_ref, cg_ref, cbn_ref, bpb_ref, g2_ref,
                         wrh_ref, wrl_ref, br_ref,
                         h1_ref, xn_ref, route_ref, *, dec_seq):
    cb = cb_ref[...]
    parts = []
    n_b = xpad_ref.shape[0]
    for t in range(dec_seq):
        acc = jnp.zeros((n_b, CONV_CH), F32)
        for jt in range(CONV_WIDTH):
            w = jnp.concatenate([cw_ref[jt]] * (n_b // SUBLANES), axis=0)
            acc = acc + w * xpad_ref[:, t + jt, :]
        parts.append(acc + cb)
    conv_y = jnp.concatenate(parts, axis=0)
    out, xn, route = _mixer_tail(conv_y, d_ref[...], gate_ref[...], x_ref[...],
                                 wpa_ref, wpb_ref, wo_ref, cg_ref, cbn_ref, bpb_ref,
                                 g2_ref, wrh_ref, wrl_ref, br_ref)
    h1_ref[...] = out
    xn_ref[...] = xn
    route_ref[...] = route


def _sample_mixer(d_t, xpad, gate_t, x_t, cw, cb, tail_w, dec_seq):
    n = x_t.shape[0]
    const = lambda i: (0, 0)
    full = lambda shape: pl.BlockSpec(shape, lambda i: (0,) * len(shape))
    return pl.pallas_call(
        functools.partial(_sample_mixer_kernel, dec_seq=dec_seq),
        grid=(1,),
        in_specs=[full((n, V_W)), full(xpad.shape), full((n, 2 * D_MODEL)),
                  full((n, D_MODEL)), full((HALO, SUBLANES, CONV_CH)), full((1, CONV_CH))]
                 + _mixer_weight_specs(const),
        out_specs=[full((n, D_MODEL)), full((n, D_MODEL)), full((n, LANES))],
        out_shape=[jax.ShapeDtypeStruct((n, D_MODEL), F32),
                   jax.ShapeDtypeStruct((n, D_MODEL), F32),
                   jax.ShapeDtypeStruct((n, LANES), F32)],
        compiler_params=_cparams(1),
        name="sample_mixer",
    )(d_t, xpad, gate_t, x_t, cw, cb, *tail_w)


def _plan_kernel(route_ref, pos_ref, meta_ref, cnt_ref, off_ref, carry_ref, *, n_meta):
    ph = pl.program_id(0)
    i = pl.program_id(1)
    lane = lax.broadcasted_iota(I32, (TP, LANES), 1)
    r = route_ref[...]
    e0 = r[:, 0:1].astype(I32)
    e1 = r[:, 1:2].astype(I32)
    oh = ((lane == e0) | (lane == e1)).astype(F32)
    tile_cnt = jnp.sum(oh, axis=0, keepdims=True)

    @pl.when((ph == 0) & (i == 0))
    def _():
        cnt_ref[...] = jnp.zeros_like(cnt_ref)

    @pl.when(ph == 0)
    def _():
        cnt_ref[...] += tile_cnt

    @pl.when((ph == 1) & (i == 0))
    def _():
        lane8 = lax.broadcasted_iota(I32, (SUBLANES, LANES), 1)
        cnt = jnp.broadcast_to(cnt_ref[...], (SUBLANES, LANES)).astype(I32)
        padded = ((cnt + (TME - 1)) >> TME_LOG) << TME_LOG
        x = padded
        for s in (1, 2, 4, 8, 16, 32, 64):
            x = x + jnp.where(lane8 >= s, pltpu.roll(x, s, 1), 0)
        off_end = x
        off_ref[...] = (off_end - padded)[0:1, :].astype(F32)
        carry_ref[...] = jnp.zeros_like(carry_ref)
        tile_row = lax.broadcasted_iota(I32, (SUBLANES, n_meta), 1) * TME
        te = jnp.zeros((SUBLANES, n_meta), I32)
        for e in range(N_EXPERTS):
            te = te + (tile_row >= off_end[:, e:e + 1]).astype(I32)
        te = jnp.minimum(te, N_EXPERTS - 1)
        n_valid = jnp.broadcast_to(off_end[:, N_EXPERTS - 1:N_EXPERTS] >> TME_LOG,
                                   (SUBLANES, n_meta))
        sub_m = lax.broadcasted_iota(I32, (SUBLANES, n_meta), 0)
        meta_ref[...] = jnp.where(sub_m == 0, te, n_valid)

    @pl.when(ph == 1)
    def _():
        rr = lax.broadcasted_iota(I32, (TP, TP), 0)
        cc = lax.broadcasted_iota(I32, (TP, TP), 1)
        tril = (cc <= rr).astype(BF16)
        cum = _dot(tril, oh.astype(BF16))
        slot = off_ref[...] + carry_ref[...] + cum - oh
        p0 = jnp.sum(jnp.where(lane == e0, slot, 0.0), axis=-1, keepdims=True)
        p1 = jnp.sum(jnp.where(lane == e1, slot, 0.0), axis=-1, keepdims=True)
        slab = jnp.where(lane == 0, p0, jnp.where(lane == 1, p1, 0.0))
        pos_ref[...] = slab.T[0:SUBLANES, :].astype(I32)
        carry_ref[...] += tile_cnt


def _plan(route2d, n_meta):
    n = route2d.shape[0]
    return pl.pallas_call(
        functools.partial(_plan_kernel, n_meta=n_meta),
        grid=(2, n // TP),
        in_specs=[pl.BlockSpec((TP, LANES), lambda ph, i: (i, 0))],
        out_specs=[pl.BlockSpec((SUBLANES, TP), lambda ph, i: (0, i * ph)),
                   pl.BlockSpec((SUBLANES, n_meta), lambda ph, i: (0, 0))],
        out_shape=[jax.ShapeDtypeStruct((SUBLANES, n), I32),
                   jax.ShapeDtypeStruct((SUBLANES, n_meta), I32)],
        scratch_shapes=[pltpu.VMEM((1, LANES), F32), pltpu.VMEM((1, LANES), F32),
                        pltpu.VMEM((1, LANES), F32)],
        compiler_params=_cparams(2),
        name="moe_plan",
    )(route2d)


def _scatter_kernel(pos_ref, xn_ref, xs_in_hbm, xs_hbm, sem):
    del xs_in_hbm

    def row_copy(i, p):
        return pltpu.make_async_copy(xn_ref.at[pl.ds(i, 1)], xs_hbm.at[pl.ds(p, 1)], sem)

    def start(i, c):
        row_copy(i, pos_ref[0, i]).start()
        row_copy(i, pos_ref[1, i]).start()
        return c

    def wait(i, c):
        row_copy(0, 0).wait()
        row_copy(0, 0).wait()
        return c

    lax.fori_loop(0, TS, start, 0, unroll=DMA_UNROLL)
    lax.fori_loop(0, TS, wait, 0, unroll=DMA_UNROLL)


def _scatter(pos, xn2d, xs_zero):
    n = xn2d.shape[0]
    return pl.pallas_call(
        _scatter_kernel,
        grid=(n // TS,),
        in_specs=[pl.BlockSpec((SUBLANES, TS), lambda i: (0, i), memory_space=pltpu.SMEM),
                  pl.BlockSpec((TS, D_MODEL), lambda i: (i, 0)),
                  pl.BlockSpec(memory_space=pl.ANY)],
        out_specs=pl.BlockSpec(memory_space=pl.ANY),
        out_shape=jax.ShapeDtypeStruct(xs_zero.shape, xs_zero.dtype),
        scratch_shapes=[pltpu.SemaphoreType.DMA],
        input_output_aliases={2: 0},
        compiler_params=_cparams(1),
        name="moe_scatter",
    )(pos, xn2d, xs_zero)


def _gmm_kernel(meta_ref, x_ref, wg_ref, wu_ref, wd_ref, y_ref):
    t = pl.program_id(0)
    n_valid = meta_ref[1, 0]

    @pl.when(t < n_valid)
    def _():
        x = x_ref[...].astype(BF16)
        a = _dot(x, wg_ref[...])
        b = _dot(x, wu_ref[...])
        hh = (a * _sigmoid(a)) * b
        y_ref[...] = _dot(hh.astype(BF16), wd_ref[...])

    @pl.when(t >= n_valid)
    def _():
        y_ref[...] = jnp.zeros_like(y_ref)


def _gmm(meta, xs, wg, wu, wd):
    n_tiles = xs.shape[0] // TME
    w_idx = lambda t, meta: (meta[0, t], 0, 0)
    grid_spec = pltpu.PrefetchScalarGridSpec(
        num_scalar_prefetch=1,
        grid=(n_tiles,),
        in_specs=[pl.BlockSpec((TME, D_MODEL), lambda t, meta: (t, 0)),
                  pl.BlockSpec((None, D_MODEL, EXPERT_FF), w_idx),
                  pl.BlockSpec((None, D_MODEL, EXPERT_FF), w_idx),
                  pl.BlockSpec((None, EXPERT_FF, D_MODEL), w_idx)],
        out_specs=pl.BlockSpec((TME, D_MODEL), lambda t, meta: (t, 0)),
    )
    return pl.pallas_call(
        _gmm_kernel,
        grid_spec=grid_spec,
        out_shape=jax.ShapeDtypeStruct(xs.shape, F32),
        compiler_params=_cparams(1),
        name="moe_gmm",
    )(meta, xs, wg, wu, wd)


def _combine_kernel(pos_ref, pos_next_ref, route_ref, h1_ref, gf_ref, ys_hbm, y_ref,
                    buf_ref, sem):
    i = pl.program_id(0)
    n = pl.num_programs(0)
    slot = i % 2

    def row_copy(s, k, r, p):
        return pltpu.make_async_copy(ys_hbm.at[pl.ds(p, 1)], buf_ref.at[s, k, pl.ds(r, 1)],
                                     sem.at[s])

    def issue(p_ref, s):
        def start(r, c):
            row_copy(s, 0, r, p_ref[0, r]).start()
            row_copy(s, 1, r, p_ref[1, r]).start()
            return c
        lax.fori_loop(0, TF, start, 0, unroll=DMA_UNROLL)

    @pl.when(i == 0)
    def _():
        issue(pos_ref, 0)

    @pl.when(i + 1 < n)
    def _():
        issue(pos_next_ref, 1 - slot)

    def wait(r, c):
        row_copy(slot, 0, 0, 0).wait()
        row_copy(slot, 1, 0, 0).wait()
        return c

    lax.fori_loop(0, TF, wait, 0, unroll=DMA_UNROLL)
    r = route_ref[...]
    out = h1_ref[...] + r[:, 2:3] * buf_ref[slot, 0] + r[:, 3:4] * buf_ref[slot, 1]
    ms = jnp.mean(out * out, axis=-1, keepdims=True)
    y_ref[...] = out * lax.rsqrt(ms + EPS) * gf_ref[...]


def _combine(pos, route2d, h1_2d, gf, ys):
    n = h1_2d.shape[0]
    n_steps = n // TF
    row = lambda i: (i, 0)
    return pl.pallas_call(
        _combine_kernel,
        grid=(n_steps,),
        in_specs=[pl.BlockSpec((SUBLANES, TF), lambda i: (0, i), memory_space=pltpu.SMEM),
                  pl.BlockSpec((SUBLANES, TF), lambda i: (0, jnp.minimum(i + 1, n_steps - 1)),
                               memory_space=pltpu.SMEM),
                  pl.BlockSpec((TF, LANES), row),
                  pl.BlockSpec((TF, D_MODEL), row),
                  pl.BlockSpec((1, D_MODEL), lambda i: (0, 0)),
                  pl.BlockSpec(memory_space=pl.ANY)],
        out_specs=pl.BlockSpec((TF, D_MODEL), row),
        out_shape=jax.ShapeDtypeStruct((n, D_MODEL), F32),
        scratch_shapes=[pltpu.VMEM((2, 2, TF, D_MODEL), F32), pltpu.SemaphoreType.DMA((2,))],
        compiler_params=_cparams(1),
        name="moe_combine",
    )(pos, pos, route2d, h1_2d, gf, ys)


def _sample_moe_kernel(xn_ref, route_ref, h1_ref, gf_ref, wg_ref, wu_ref, wd_ref,
                       y_ref, acc_ref):
    e = pl.program_id(0)

    @pl.when(e == 0)
    def _():
        acc_ref[...] = jnp.zeros_like(acc_ref)

    r = route_ref[...]
    ef = e.astype(F32)
    gate = jnp.where(r[:, 0:1] == ef, r[:, 2:3], 0.0) + jnp.where(r[:, 1:2] == ef, r[:, 3:4], 0.0)
    x = xn_ref[...].astype(BF16)
    a = _dot(x, wg_ref[...])
    b = _dot(x, wu_ref[...])
    hh = (a * _sigmoid(a)) * b
    acc_ref[...] += gate * _dot(hh.astype(BF16), wd_ref[...])

    @pl.when(e == N_EXPERTS - 1)
    def _():
        out = h1_ref[...] + acc_ref[...]
        ms = jnp.mean(out * out, axis=-1, keepdims=True)
        y_ref[...] = out * lax.rsqrt(ms + EPS) * gf_ref[...]


def _sample_moe(xn, route, h1, gf, wg, wu, wd):
    n = xn.shape[0]
    const = lambda e: (0, 0)
    w_idx = lambda e: (e, 0, 0)
    return pl.pallas_call(
        _sample_moe_kernel,
        grid=(N_EXPERTS,),
        in_specs=[pl.BlockSpec((n, D_MODEL), const),
                  pl.BlockSpec((n, LANES), const),
                  pl.BlockSpec((n, D_MODEL), const),
                  pl.BlockSpec((1, D_MODEL), const),
                  pl.BlockSpec((None, D_MODEL, EXPERT_FF), w_idx),
                  pl.BlockSpec((None, D_MODEL, EXPERT_FF), w_idx),
                  pl.BlockSpec((None, EXPERT_FF, D_MODEL), w_idx)],
        out_specs=pl.BlockSpec((n, D_MODEL), const),
        out_shape=jax.ShapeDtypeStruct((n, D_MODEL), F32),
        scratch_shapes=[pltpu.VMEM((n, D_MODEL), F32)],
        compiler_params=_cparams(1),
        name="sample_moe",
    )(xn, route, h1, gf, wg, wu, wd)


def kernel(x_prompt, x_sample, cache_k, cache_v, state_conv, page_table, norm1_g, w_in, b_gate,
           lambda_q1, lambda_k1, lambda_q2, lambda_k2, attn_norm_g, w_pa, conv_w, conv_b,
           conv_norm_g, conv_norm_b, w_pb, b_pb, w_o, norm2_g, w_router_group, b_router_group,
           w_router_expert, b_router_expert, w_exp_gate, w_exp_up, w_exp_down, final_norm_g):
    depth = w_in.shape[0]
    assert depth == 1, "kernels are written for a single layer"
    b, s, _ = x_prompt.shape
    bd, dec_seq, _ = x_sample.shape
    n_pages = page_table.shape[1]
    past_len = n_pages * PAGE_SIZE
    n_p = b * s
    n_s = bd * dec_seq
    assert s % TM_IN == 0 and s % TC == 0 and s % TQ == 0 and n_p % TS == 0
    assert n_pages % PG == 0 and dec_seq <= SUBLANES
    l = 0
    lam_init = 0.8 - 0.6 * math.exp(-0.3 * l)

    row = lambda a: a.reshape(1, -1)
    g1 = row(norm1_g[l])
    w_in_bf = w_in[l].astype(BF16)
    bg = row(b_gate[l])
    lams = (row(lambda_q1[l]), row(lambda_k1[l]), row(lambda_q2[l]), row(lambda_k2[l]))
    g_attn = row(attn_norm_g[l])
    cw = jnp.broadcast_to(jnp.pad(conv_w[l], ((0, HALO - CONV_WIDTH), (0, 0)))[:, None, :],
                          (HALO, SUBLANES, CONV_CH))
    cb = row(conv_b[l])
    w_r = jnp.pad(jnp.concatenate([w_router_group[l], w_router_expert[l]], axis=1),
                  ((0, 0), (0, LANES - N_GROUPS - N_EXPERTS)))
    b_r = jnp.pad(jnp.concatenate([b_router_group[l], b_router_expert[l]]),
                  (0, LANES - N_GROUPS - N_EXPERTS)).reshape(1, LANES)
    w_r_hi = w_r.astype(BF16)
    w_r_lo = (w_r - w_r_hi.astype(F32)).astype(BF16)
    tail_w = (w_pa[l].astype(BF16), w_pb[l].astype(BF16), w_o[l].astype(BF16),
              row(conv_norm_g[l]), row(conv_norm_b[l]), row(b_pb[l]), row(norm2_g[l]),
              w_r_hi, w_r_lo, b_r)
    wg = w_exp_gate[l].astype(BF16)
    wu = w_exp_up[l].astype(BF16)
    wd = w_exp_down[l].astype(BF16)
    gf = row(final_norm_g)

    cos_p, sa_p, sb_p = _rope_tables(jnp.arange(s, dtype=F32))
    q_p, k_p, v_p, u_p, gate_p = _inproj(x_prompt.reshape(n_p, D_MODEL), g1, w_in_bf, bg,
                                         cos_p, sa_p, sb_p, TM_IN, s // TM_IN)
    k_p3 = k_p.reshape(b, s, QK_W)
    v_p3 = v_p.reshape(b, s, V_W)
    u_p3 = u_p.reshape(b, s, CONV_CH)
    d_p = _prompt_attn(lams, g_attn, q_p.reshape(b, s, QK_W), k_p3, v_p3, lam_init)
    h1_p, xn_p, route_p = _prompt_mixer(d_p, u_p3, gate_p.reshape(b, s, 2 * D_MODEL),
                                        x_prompt, cw, cb, tail_w)
    route_p2 = route_p.reshape(n_p, LANES)
    n_slots = 2 * n_p + N_EXPERTS * TME
    n_meta = pl.cdiv(n_slots // TME, LANES) * LANES
    pos_p, meta_p = _plan(route_p2, n_meta)
    xs = _scatter(pos_p, xn_p.reshape(n_p, D_MODEL), jnp.zeros((n_slots, D_MODEL), F32))
    ys = _gmm(meta_p, xs, wg, wu, wd)
    y_prompt = _combine(pos_p, route_p2, h1_p.reshape(n_p, D_MODEL), gf, ys)
    y_prompt = y_prompt.reshape(b, s, D_MODEL)

    pos_s = past_len + jnp.arange(dec_seq, dtype=F32)
    cos_s, sa_s, sb_s = (jnp.tile(t, (bd, 1)) for t in _rope_tables(pos_s))
    q_s, k_s, v_s, u_s, gate_s = _inproj(x_sample.reshape(n_s, D_MODEL), g1, w_in_bf, bg,
                                         cos_s, sa_s, sb_s, n_s, 1)
    q4 = q_s.reshape(bd, dec_seq, N_HEADS, 2, HEAD_DIM)
    qt = jnp.transpose(q4, (0, 3, 1, 2, 4))
    mask = (jnp.arange(N_HEADS)[None, :, None, None] == jnp.arange(N_HEADS)[None, None, :, None]) \
        & (jnp.arange(2)[:, None, None, None] == jnp.arange(2)[None, None, None, :])
    qbd = jnp.where(mask[None, :, None, :, :, :, None], qt[:, :, :, :, None, None, :],
                    jnp.zeros((), BF16))
    qbd = qbd.reshape(bd, 2 * dec_seq * N_HEADS, QK_W)
    n_pool = cache_k.shape[1]
    pad8 = lambda a: jnp.pad(a.reshape(bd, dec_seq, -1), ((0, 0), (0, SUBLANES - dec_seq), (0, 0)))
    d_s = _sample_attn(page_table + l * n_pool, lams, g_attn, qbd, pad8(k_s), pad8(v_s),
                       cache_k.reshape(depth * n_pool, PAGE_SIZE, QK_W),
                       cache_v.reshape(depth * n_pool, PAGE_SIZE, V_W), dec_seq, lam_init)
    d_s = d_s[:, :dec_seq, :]
    tmaj = lambda a: jnp.transpose(a.reshape(bd, dec_seq, -1), (1, 0, 2)).reshape(n_s, -1)
    xpad = jnp.concatenate([state_conv[l], u_s.reshape(bd, dec_seq, CONV_CH)], axis=1)
    h1_s, xn_s, route_s = _sample_mixer(tmaj(d_s), xpad, tmaj(gate_s), tmaj(x_sample),
                                        cw, cb, tail_w, dec_seq)
    y_s = _sample_moe(xn_s, route_s, h1_s, gf, wg, wu, wd)
    y_sample = jnp.transpose(y_s.reshape(dec_seq, bd, D_MODEL), (1, 0, 2))

    k_prompt = k_p.reshape(1, b, s, N_HEADS, 2 * HEAD_DIM)
    v_prompt = v_p.reshape(1, b, s, N_HEADS, V_DIM)
    conv_prompt = u_p3[:, s - (CONV_WIDTH - 1):, :][None]
    k_sample = k_s.reshape(1, bd, dec_seq, N_HEADS, 2 * HEAD_DIM)
    v_sample = v_s.reshape(1, bd, dec_seq, N_HEADS, V_DIM)
    conv_sample = xpad[:, dec_seq:, :][None]
    return (y_prompt, y_sample, k_prompt, v_prompt, conv_prompt, k_sample, v_sample, conv_sample)
```
